```python
import jax, jax.numpy as jnp
from jax import lax
import numpy as np

D_MODEL = 2048
BATCH = 4
SEQ = 2048
DEPTH = 1

PLE_DIM = 256
MLA_HEADS = 8
QK_NOPE = 128
QK_ROPE = 64
V_HEAD = 128
Q_LORA = 512
KV_LORA = 256
ROPE_THETA = 10000.0
Q_BLOCK = 128
SGU_GROUPS = 8
SGU_GROUP_DIM = 128
CHUNK = 128
MLA_WIDTH = MLA_HEADS * V_HEAD
SGU_WIDTH = SGU_GROUPS * SGU_GROUP_DIM
MIX_WIDTH = MLA_WIDTH + SGU_WIDTH
OFF_CKV = Q_LORA
OFF_KR = Q_LORA + KV_LORA
OFF_U = OFF_KR + QK_ROPE
OFF_V = OFF_U + SGU_WIDTH
IN_COLS = OFF_V + SGU_WIDTH
PEER_HEADS = 8
PEER_KEYS = 128
PEER_QDIM = 256
PEER_TOPK = 16
N_EXPERTS = PEER_KEYS * PEER_KEYS
EPS = 1e-6

kernel_name = "hymba_mla_sgu_peer_ple_block"


def rmsnorm(x, g):
    xf = x.astype(jnp.float32)
    y = xf * lax.rsqrt(jnp.mean(xf * xf, axis=-1, keepdims=True) + EPS)
    return (y * g.astype(jnp.float32)).astype(x.dtype)


def layernorm(x, g, b):
    xf = x.astype(jnp.float32)
    mu = jnp.mean(xf, axis=-1, keepdims=True)
    xc = xf - mu
    y = xc * lax.rsqrt(jnp.mean(xc * xc, axis=-1, keepdims=True) + EPS)
    return (y * g.astype(jnp.float32) + b.astype(jnp.float32)).astype(x.dtype)


def rope(x, pos):
    half = x.shape[-1] // 2
    freqs = ROPE_THETA ** (-jnp.arange(half, dtype=jnp.float32) / half)
    ang = pos.astype(jnp.float32)[..., None] * freqs
    cos = jnp.cos(ang)[:, :, None, :]
    sin = jnp.sin(ang)[:, :, None, :]
    xf = x.astype(jnp.float32)
    x1, x2 = xf[..., :half], xf[..., half:]
    out = jnp.concatenate([x1 * cos - x2 * sin, x2 * cos + x1 * sin], axis=-1)
    return out.astype(x.dtype)


def causal_block_attention(q, k, v):
    B, S, H, Dq = q.shape
    nb = S // Q_BLOCK
    scale = Dq ** -0.5
    qb = jnp.moveaxis(q.reshape(B, nb, Q_BLOCK, H, Dq), 1, 0)
    k_pos = jnp.arange(S)

    def one(args):
        q_blk, start = args
        s = jnp.einsum('bqhd,bkhd->bhqk', q_blk, k, preferred_element_type=jnp.float32) * scale
        q_pos = start + jnp.arange(Q_BLOCK)
        mask = k_pos[None, :] <= q_pos[:, None]
        s = jnp.where(mask[None, None], s, -jnp.inf)
        pr = jax.nn.softmax(s, axis=-1)
        return jnp.einsum('bhqk,bkhd->bqhd', pr.astype(v.dtype), v)

    starts = jnp.arange(nb) * Q_BLOCK
    out = lax.map(one, (qb, starts))
    return jnp.moveaxis(out, 0, 1).reshape(B, S, H, v.shape[-1])


def mla(c_q, c_kv, k_r, pos, g_cq, w_uq, g_ckv, w_ukv):
    B, S, _ = c_q.shape
    q = (rmsnorm(c_q, g_cq) @ w_uq).reshape(B, S, MLA_HEADS, QK_NOPE + QK_ROPE)
    kv = (rmsnorm(c_kv, g_ckv) @ w_ukv).reshape(B, S, MLA_HEADS, QK_NOPE + V_HEAD)
    q_nope, q_rope = q[..., :QK_NOPE], rope(q[..., QK_NOPE:], pos)
    k_nope, v = kv[..., :QK_NOPE], kv[..., QK_NOPE:]
    k_rope = rope(k_r[:, :, None, :], pos)
    q = jnp.concatenate([q_nope, q_rope], axis=-1)
    k = jnp.concatenate([k_nope, jnp.broadcast_to(k_rope, (B, S, MLA_HEADS, QK_ROPE))], axis=-1)
    o = causal_block_attention(q, k, v)
    return o.reshape(B, S, MLA_WIDTH)


def chunked_sgu(u, v, g_ln, b_ln, w_s, b_s):
    B, S, _ = u.shape
    nc = S // CHUNK
    u = jax.nn.gelu(u).reshape(B, nc, CHUNK, SGU_GROUPS, SGU_GROUP_DIM)
    v = layernorm(jax.nn.gelu(v).reshape(B, S, SGU_GROUPS, SGU_GROUP_DIM), g_ln, b_ln)
    v = v.reshape(B, nc, CHUNK, SGU_GROUPS, SGU_GROUP_DIM)
    w = w_s * jnp.tril(jnp.ones((CHUNK, CHUNK), dtype=w_s.dtype))[None]
    mixed = jnp.einsum('gts,bcsgd->bctgd', w, v) + jnp.transpose(b_s)[None, None, :, :, None]
    return (u * mixed).reshape(B, S, SGU_WIDTH)


def peer(a, w_q, k1, k2, u_exp, v_exp):
    B, S, D = a.shape
    half = PEER_QDIM // 2
    q = (a @ w_q).reshape(B, S, PEER_HEADS, PEER_QDIM)
    s1 = jnp.einsum('bshd,hnd->bshn', q[..., :half], k1, preferred_element_type=jnp.float32)
    s2 = jnp.einsum('bshd,hnd->bshn', q[..., half:], k2, preferred_element_type=jnp.float32)
    t1, i1 = lax.top_k(s1, PEER_TOPK)
    t2, i2 = lax.top_k(s2, PEER_TOPK)
    cand = (t1[..., :, None] + t2[..., None, :]).reshape(B, S, PEER_HEADS, PEER_TOPK * PEER_TOPK)
    cand_idx = (i1[..., :, None] * PEER_KEYS + i2[..., None, :]).reshape(B, S, PEER_HEADS, PEER_TOPK * PEER_TOPK)
    top, sel = lax.top_k(cand, PEER_TOPK)
    experts = jnp.take_along_axis(cand_idx, sel, axis=-1)
    gates = jax.nn.softmax(top, axis=-1)
    nc = S // CHUNK

    def to_chunks(t):
        return jnp.moveaxis(t.reshape((B, nc, CHUNK) + t.shape[2:]), 1, 0)

    def one(args):
        a_c, e_c, g_c = args
        u_sel = jnp.take(u_exp, e_c, axis=0)
        h = jnp.einsum('bthkd,btd->bthk', u_sel, a_c, preferred_element_type=jnp.float32)
        w = (g_c * jax.nn.gelu(h)).astype(a_c.dtype)
        v_sel = jnp.take(v_exp, e_c, axis=0)
        return jnp.einsum('bthk,bthkd->btd', w, v_sel)

    out = lax.map(one, (to_chunks(a), to_chunks(experts), to_chunks(gates)))
    return jnp.moveaxis(out, 0, 1).reshape(B, S, D)


def setup_inputs(seed: int = 0) -> dict:
    key = jax.random.key(seed)
    ks = iter(jax.random.split(key, 40))

    def nrm(shape, scale):
        return jax.random.normal(next(ks), shape, dtype=jnp.float32) * scale

    def gain(shape):
        return 1.0 + nrm(shape, 0.02)

    L = DEPTH
    x = nrm((BATCH, SEQ, D_MODEL), 1.0)
    p = nrm((DEPTH, BATCH, SEQ, PLE_DIM), 1.0)
    positions = (jax.random.randint(next(ks), (BATCH, 1), 0, 4096, dtype=jnp.int32)
                 + jnp.arange(SEQ, dtype=jnp.int32)[None, :])
    return {
        "x": x,
        "p": p,
        "positions": positions,
        "g_mix": gain((L, D_MODEL)),
        "w_in": nrm((L, D_MODEL, IN_COLS), D_MODEL ** -0.5),
        "g_cq": gain((L, Q_LORA)),
        "w_uq": nrm((L, Q_LORA, MLA_HEADS * (QK_NOPE + QK_ROPE)), Q_LORA ** -0.5),
        "g_ckv": gain((L, KV_LORA)),
        "w_ukv": nrm((L, KV_LORA, MLA_HEADS * (QK_NOPE + V_HEAD)), KV_LORA ** -0.5),
        "g_ln_sgu": gain((L, SGU_GROUPS, SGU_GROUP_DIM)),
        "b_ln_sgu": nrm((L, SGU_GROUPS, SGU_GROUP_DIM), 0.02),
        "w_sgu": nrm((L, SGU_GROUPS, CHUNK, CHUNK), CHUNK ** -0.5),
        "b_sgu": gain((L, SGU_GROUPS, CHUNK)),
        "g_out_attn": gain((L, MLA_WIDTH)),
        "g_out_sgu": gain((L, SGU_WIDTH)),
        "w_out": nrm((L, MIX_WIDTH, D_MODEL), MIX_WIDTH ** -0.5),
        "g_ffn": gain((L, D_MODEL)),
        "w_peer_q": nrm((L, D_MODEL, PEER_HEADS * PEER_QDIM), D_MODEL ** -0.5),
        "peer_k1": nrm((L, PEER_HEADS, PEER_KEYS, PEER_QDIM // 2), (PEER_QDIM // 2) ** -0.5),
        "peer_k2": nrm((L, PEER_HEADS, PEER_KEYS, PEER_QDIM // 2), (PEER_QDIM // 2) ** -0.5),
        "peer_u": nrm((L, N_EXPERTS, D_MODEL), D_MODEL ** -0.5),
        "peer_v": nrm((L, N_EXPERTS, D_MODEL), PEER_HEADS ** -0.5),
        "g_ple": gain((L, D_MODEL)),
        "w_ple_gate": nrm((L, D_MODEL, D_MODEL), D_MODEL ** -0.5),
        "w_ple_proj": nrm((L, PLE_DIM, D_MODEL), PLE_DIM ** -0.5),
        "g_final": gain((D_MODEL,)),
    }


def reference(x, p, positions, g_mix, w_in, g_cq, w_uq, g_ckv, w_ukv, g_ln_sgu, b_ln_sgu,
              w_sgu, b_sgu, g_out_attn, g_out_sgu, w_out, g_ffn, w_peer_q, peer_k1, peer_k2,
              peer_u, peer_v, g_ple, w_ple_gate, w_ple_proj, g_final):
    h = x
    for i in range(DEPTH):
        z = rmsnorm(h, g_mix[i]) @ w_in[i]
        c_q = z[..., :OFF_CKV]
        c_kv = z[..., OFF_CKV:OFF_KR]
        k_r = z[..., OFF_KR:OFF_U]
        u = z[..., OFF_U:OFF_V]
        v = z[..., OFF_V:]
        o_attn = rmsnorm(mla(c_q, c_kv, k_r, positions, g_cq[i], w_uq[i], g_ckv[i], w_ukv[i]),
                         g_out_attn[i])
        o_sgu = rmsnorm(chunked_sgu(u, v, g_ln_sgu[i], b_ln_sgu[i], w_sgu[i], b_sgu[i]),
                        g_out_sgu[i])
        h = h + jnp.concatenate([o_attn, o_sgu], axis=-1) @ w_out[i]
        h = h + peer(rmsnorm(h, g_ffn[i]), w_peer_q[i], peer_k1[i], peer_k2[i], peer_u[i], peer_v[i])
        gate = jax.nn.sigmoid(rmsnorm(h, g_ple[i]) @ w_ple_gate[i])
        h = h + gate * (p[i] @ w_ple_proj[i])
    return rmsnorm(h, g_final)
```

```python
import functools

import jax
import jax.numpy as jnp
from jax import lax
from jax.experimental import pallas as pl
from jax.experimental.pallas import tpu as pltpu

F32 = jnp.float32
BF16 = jnp.bfloat16

EPS = 1e-6
ROPE_THETA = 10000.0
LANES = 128
VMEM_LIMIT = 56 * 1024 * 1024

MLA_HEADS = 8
QK_NOPE = 128
QK_ROPE = 64
V_HEAD = 128
Q_LORA = 512
KV_LORA = 256
HEAD_PAD = 256
SGU_GROUPS = 8
SGU_GROUP_DIM = 128
CHUNK = 128
PEER_HEADS = 8
PEER_KEYS = 128
PEER_TOPK = 16
NEG = -1e30

NT_DIMS = (((1,), (1,)), ((), ()))


def _rms(x, g):
    return x * lax.rsqrt(jnp.mean(x * x, axis=-1, keepdims=True) + EPS) * g


def _const_spec(shape):
    n = len(shape)
    return pl.BlockSpec(shape, lambda *_: (0,) * n, pipeline_mode=pl.Buffered(1))


def _params(*sem):
    return pltpu.CompilerParams(dimension_semantics=sem, vmem_limit_bytes=VMEM_LIMIT)


def _in_proj_kernel(x_ref, pos_ref, gmix_ref, win_ref, gcq_ref, wuq_ref, gckv_ref, wukv_ref,
                    freq_ref, sign_ref, q_ref, k_ref, v_ref, u_ref, vv_ref, *, q_scale):
    c_u = Q_LORA + KV_LORA
    sgu_w = SGU_GROUPS * SGU_GROUP_DIM
    c_v = c_u + sgu_w
    c_kr = c_v + sgu_w
    xn = _rms(x_ref[...], gmix_ref[...])
    z = jnp.dot(xn.astype(BF16), win_ref[...], preferred_element_type=F32)
    u_ref[...] = z[:, c_u:c_v]
    vv_ref[...] = z[:, c_v:c_kr]

    ang = pos_ref[...] * freq_ref[...]
    cos = jnp.cos(ang)
    ssin = jnp.sin(ang) * sign_ref[...]

    def rope(blk):
        return blk * cos + pltpu.roll(blk, LANES // 2, 1) * ssin

    cqn = _rms(z[:, :Q_LORA], gcq_ref[...])
    q = jnp.dot(cqn.astype(BF16), wuq_ref[...], preferred_element_type=F32)
    ckvn = _rms(z[:, Q_LORA:c_u], gckv_ref[...])
    kv = jnp.dot(ckvn.astype(BF16), wukv_ref[...], preferred_element_type=F32)
    kr = rope(z[:, c_kr:c_kr + LANES]).astype(BF16)
    k_width = MLA_HEADS * QK_NOPE
    v_ref[...] = kv[:, k_width:].astype(BF16)
    for h in range(MLA_HEADS):
        o = h * HEAD_PAD
        q_ref[:, o:o + QK_NOPE] = (q[:, o:o + QK_NOPE] * q_scale).astype(BF16)
        q_ref[:, o + QK_NOPE:o + HEAD_PAD] = (rope(q[:, o + QK_NOPE:o + HEAD_PAD]) * q_scale).astype(BF16)
        k_ref[:, o:o + QK_NOPE] = kv[:, h * QK_NOPE:(h + 1) * QK_NOPE].astype(BF16)
        k_ref[:, o + QK_NOPE:o + HEAD_PAD] = kr


def _in_proj(x, pos, g_mix, w_in, g_cq, w_uq, g_ckv, w_ukv, freq, sign, tm):
    n, d = x.shape
    qk_w = MLA_HEADS * HEAD_PAD
    v_w = MLA_HEADS * V_HEAD
    sgu_w = SGU_GROUPS * SGU_GROUP_DIM
    row = lambda w: pl.BlockSpec((tm, w), lambda i: (i, 0))
    return pl.pallas_call(
        functools.partial(_in_proj_kernel, q_scale=float(QK_NOPE + QK_ROPE) ** -0.5),
        grid=(n // tm,),
        in_specs=[row(d), row(1), _const_spec(g_mix.shape), _const_spec(w_in.shape),
                  _const_spec(g_cq.shape), _const_spec(w_uq.shape), _const_spec(g_ckv.shape),
                  _const_spec(w_ukv.shape), _const_spec(freq.shape), _const_spec(sign.shape)],
        out_specs=[row(qk_w), row(qk_w), row(v_w), row(sgu_w), row(sgu_w)],
        out_shape=[jax.ShapeDtypeStruct((n, qk_w), BF16), jax.ShapeDtypeStruct((n, qk_w), BF16),
                   jax.ShapeDtypeStruct((n, v_w), BF16), jax.ShapeDtypeStruct((n, sgu_w), F32),
                   jax.ShapeDtypeStruct((n, sgu_w), F32)],
        compiler_params=_params("parallel"),
        name="in_proj",
    )(x, pos, g_mix, w_in, g_cq, w_uq, g_ckv, w_ukv, freq, sign)


def _attn_kernel(q_ref, k_ref, v_ref, o_ref, *, tq, tk):
    qi = pl.program_id(2)
    q = q_ref[...]
    row = qi * tq + lax.broadcasted_iota(jnp.int32, (tq, tk), 0)
    col0 = lax.broadcasted_iota(jnp.int32, (tq, tk), 1)

    def body(kb, carry):
        m, l, acc = carry
        start = pl.multiple_of(kb * tk, tk)
        k = k_ref[pl.ds(start, tk), :]
        v = v_ref[pl.ds(start, tk), :]
        s = lax.dot_general(q, k, NT_DIMS, preferred_element_type=F32)
        s = jnp.where(col0 + start <= row, s, NEG)
        m_new = jnp.maximum(m, jnp.max(s, axis=-1, keepdims=True))
        p = jnp.exp(s - m_new)
        alpha = jnp.exp(m - m_new)
        l = alpha * l + jnp.sum(p, axis=-1, keepdims=True)
        acc = alpha * acc + jnp.dot(p.astype(BF16), v, preferred_element_type=F32)
        return m_new, l, acc

    init = (jnp.full((tq, 1), NEG, F32), jnp.zeros((tq, 1), F32), jnp.zeros((tq, V_HEAD), F32))
    _, l, acc = lax.fori_loop(0, (qi + 1) * (tq // tk), body, init)
    o_ref[...] = acc / l


def _attention(q, k, v, batch, seq, tq, tk):
    n = q.shape[0]
    qblocks = seq // tq
    return pl.pallas_call(
        functools.partial(_attn_kernel, tq=tq, tk=tk),
        grid=(batch, MLA_HEADS, qblocks),
        in_specs=[pl.BlockSpec((tq, HEAD_PAD), lambda b, h, i: (b * qblocks + i, h)),
                  pl.BlockSpec((seq, HEAD_PAD), lambda b, h, i: (b, h)),
                  pl.BlockSpec((seq, V_HEAD), lambda b, h, i: (b, h))],
        out_specs=pl.BlockSpec((tq, V_HEAD), lambda b, h, i: (b * qblocks + i, h)),
        out_shape=jax.ShapeDtypeStruct((n, MLA_HEADS * V_HEAD), F32),
        compiler_params=_params("parallel", "parallel", "arbitrary"),
        name="attention",
    )(q, k, v)


def _sgu_kernel(u_ref, v_ref, w_ref, bcol_ref, gln_ref, bln_ref, o_ref):
    t = lax.broadcasted_iota(jnp.int32, (CHUNK, CHUNK), 0)
    s = lax.broadcasted_iota(jnp.int32, (CHUNK, CHUNK), 1)
    causal = s <= t
    for g in range(SGU_GROUPS):
        sl = slice(g * SGU_GROUP_DIM, (g + 1) * SGU_GROUP_DIM)
        vg = jax.nn.gelu(v_ref[:, sl])
        mu = jnp.mean(vg, axis=-1, keepdims=True)
        vc = vg - mu
        vn = vc * lax.rsqrt(jnp.mean(vc * vc, axis=-1, keepdims=True) + EPS)
        vn = vn * gln_ref[g:g + 1, :] + bln_ref[g:g + 1, :]
        w = jnp.where(causal, w_ref[g], 0.0).astype(BF16)
        mixed = jnp.dot(w, vn.astype(BF16), preferred_element_type=F32) + bcol_ref[:, g:g + 1]
        o_ref[:, sl] = jax.nn.gelu(u_ref[:, sl]) * mixed


def _sgu(u, v, w_s, b_col, g_ln, b_ln):
    n, width = u.shape
    row = pl.BlockSpec((CHUNK, width), lambda i: (i, 0))
    return pl.pallas_call(
        _sgu_kernel,
        grid=(n // CHUNK,),
        in_specs=[row, row, _const_spec(w_s.shape), _const_spec(b_col.shape),
                  _const_spec(g_ln.shape), _const_spec(b_ln.shape)],
        out_specs=row,
        out_shape=jax.ShapeDtypeStruct((n, width), F32),
        compiler_params=_params("parallel"),
        name="sgu",
    )(u, v, w_s, b_col, g_ln, b_ln)


def _out_proj_kernel(x_ref, oa_ref, os_ref, ga_ref, gs_ref, wout_ref, gffn_ref, wq_ref,
                     k1_ref, k2_ref, h_ref, a_ref, s1_ref, s2_ref):
    oa = _rms(oa_ref[...], ga_ref[...]).astype(BF16)
    os_ = _rms(os_ref[...], gs_ref[...]).astype(BF16)
    mix = jnp.concatenate([oa, os_], axis=-1)
    h = x_ref[...] + jnp.dot(mix, wout_ref[...], preferred_element_type=F32)
    h_ref[...] = h
    a = _rms(h, gffn_ref[...]).astype(BF16)
    a_ref[...] = a
    qp = jnp.dot(a, wq_ref[...], preferred_element_type=F32).astype(BF16)
    half = PEER_KEYS
    for hd in range(PEER_HEADS):
        o = hd * 2 * half
        s1_ref[hd] = lax.dot_general(k1_ref[hd], qp[:, o:o + half], NT_DIMS,
                                     preferred_element_type=F32)
        s2_ref[hd] = lax.dot_general(k2_ref[hd], qp[:, o + half:o + 2 * half], NT_DIMS,
                                     preferred_element_type=F32)


def _out_proj(x, oa, os_, g_a, g_s, w_out, g_ffn, w_q, k1, k2, tm):
    n, d = x.shape
    row = lambda w: pl.BlockSpec((tm, w), lambda i: (i, 0))
    sc = pl.BlockSpec((PEER_HEADS, PEER_KEYS, tm), lambda i: (0, 0, i))
    return pl.pallas_call(
        _out_proj_kernel,
        grid=(n // tm,),
        in_specs=[row(d), row(oa.shape[1]), row(os_.shape[1]), _const_spec(g_a.shape),
                  _const_spec(g_s.shape), _const_spec(w_out.shape), _const_spec(g_ffn.shape),
                  _const_spec(w_q.shape), _const_spec(k1.shape), _const_spec(k2.shape)],
        out_specs=[row(d), row(d), sc, sc],
        out_shape=[jax.ShapeDtypeStruct((n, d), F32), jax.ShapeDtypeStruct((n, d), BF16),
                   jax.ShapeDtypeStruct((PEER_HEADS, PEER_KEYS, n), F32),
                   jax.ShapeDtypeStruct((PEER_HEADS, PEER_KEYS, n), F32)],
        compiler_params=_params("parallel"),
        name="out_proj",
    )(x, oa, os_, g_a, g_s, w_out, g_ffn, w_q, k1, k2)


def _top_values(s, k):
    rows = []
    for _ in range(k):
        m = jnp.max(s, axis=0, keepdims=True)
        rows.append(m)
        s = jnp.where(s == m, NEG, s)
    return jnp.concatenate(rows, axis=0)


def _peer_topk_kernel(s1_ref, s2_ref, tau_ref, c_ref, e2_ref):
    s1 = s1_ref[0]
    s2 = s2_ref[0]
    t1 = _top_values(s1, PEER_TOPK)
    t2 = _top_values(s2, PEER_TOPK)
    cand = jnp.concatenate([t1[a:a + 1, :] + t2 for a in range(PEER_TOPK)], axis=0)
    tau = _top_values(cand, PEER_TOPK)[PEER_TOPK - 1:PEER_TOPK, :]
    m1 = t1[0:1, :]
    m2 = t2[0:1, :]
    e1t = jnp.exp(t1 - m1)
    e2t = jnp.exp(t2 - m2)
    prod = jnp.concatenate([e1t[a:a + 1, :] * e2t for a in range(PEER_TOPK)], axis=0)
    z = jnp.sum(jnp.where(cand >= tau, prod, 0.0), axis=0, keepdims=True)
    tau_ref[0] = jnp.broadcast_to(tau, tau_ref.shape[1:])
    c_ref[0] = jnp.exp(s1 - m1) / z
    e2_ref[0] = jnp.exp(s2 - m2)


def _peer_topk(s1, s2, tt):
    heads, keys, n = s1.shape
    blk = pl.BlockSpec((1, keys, tt), lambda h, i: (h, 0, i))
    tau_blk = pl.BlockSpec((1, 8, tt), lambda h, i: (h, 0, i))
    return pl.pallas_call(
        _peer_topk_kernel,
        grid=(heads, n // tt),
        in_specs=[blk, blk],
        out_specs=[tau_blk, blk, blk],
        out_shape=[jax.ShapeDtypeStruct((heads, 8, n), F32),
                   jax.ShapeDtypeStruct((heads, keys, n), F32),
                   jax.ShapeDtypeStruct((heads, keys, n), F32)],
        compiler_params=_params("parallel", "parallel"),
        name="peer_topk",
    )(s1, s2)


def _peer_kernel(a_ref, u_ref, v_ref, s1_ref, c_ref, s2_ref, e2_ref, tau_ref, o_ref,
                 p_ref, *, tm, te):
    j = pl.program_id(1)

    @pl.when(j == 0)
    def _():
        o_ref[...] = jnp.zeros_like(o_ref)

    ht = lax.dot_general(u_ref[...], a_ref[...], NT_DIMS, preferred_element_type=F32)
    keys_per_step = te // PEER_KEYS
    key0 = pl.multiple_of(j * keys_per_step, keys_per_step)
    for r in range(keys_per_step):
        for c in range(tm // LANES):
            ls = slice(c * LANES, (c + 1) * LANES)
            w = jnp.zeros((PEER_KEYS, LANES), F32)
            for hd in range(PEER_HEADS):
                s1row = s1_ref[hd, pl.ds(key0, keys_per_step), ls][r:r + 1, :]
                crow = c_ref[hd, pl.ds(key0, keys_per_step), ls][r:r + 1, :]
                keep = (s2_ref[hd, :, ls] + s1row) >= tau_ref[hd, 0:1, ls]
                w = w + jnp.where(keep, e2_ref[hd, :, ls] * crow, 0.0)
            pt = w * jax.nn.gelu(ht[r * PEER_KEYS:(r + 1) * PEER_KEYS, ls])
            p_ref[ls, r * PEER_KEYS:(r + 1) * PEER_KEYS] = pt.T.astype(BF16)
    o_ref[...] += jnp.dot(p_ref[...], v_ref[...], preferred_element_type=F32)


def _peer(a, u, v, s1, c, s2, e2, tau, tm, te):
    n, d = a.shape
    n_exp = u.shape[0]
    tab = pl.BlockSpec((PEER_HEADS, PEER_KEYS, tm), lambda i, j: (0, 0, i))
    return pl.pallas_call(
        functools.partial(_peer_kernel, tm=tm, te=te),
        grid=(n // tm, n_exp // te),
        in_specs=[pl.BlockSpec((tm, d), lambda i, j: (i, 0)),
                  pl.BlockSpec((te, d), lambda i, j: (j, 0)),
                  pl.BlockSpec((te, d), lambda i, j: (j, 0)),
                  tab, tab, tab, tab,
                  pl.BlockSpec((PEER_HEADS, 8, tm), lambda i, j: (0, 0, i))],
        out_specs=pl.BlockSpec((tm, d), lambda i, j: (i, 0)),
        out_shape=jax.ShapeDtypeStruct((n, d), F32),
        scratch_shapes=[pltpu.VMEM((tm, te), BF16)],
        compiler_params=_params("parallel", "arbitrary"),
        name="peer",
    )(a, u, v, s1, c, s2, e2, tau)


def _ple_kernel(h_ref, f_ref, p_ref, gple_ref, wg_ref, wp_ref, o_ref):
    h = h_ref[...] + f_ref[...]
    gate = jax.nn.sigmoid(jnp.dot(_rms(h, gple_ref[...]).astype(BF16), wg_ref[...],
                                  preferred_element_type=F32))
    proj = jnp.dot(p_ref[...].astype(BF16), wp_ref[...], preferred_element_type=F32)
    o_ref[...] = h + gate * proj


def _final_norm_kernel(h_ref, g_ref, o_ref):
    o_ref[...] = _rms(h_ref[...], g_ref[...])


def _ple(h, f, p, g_ple, w_gate, w_proj, tm):
    n, d = h.shape
    row = lambda w: pl.BlockSpec((tm, w), lambda i: (i, 0))
    return pl.pallas_call(
        _ple_kernel,
        grid=(n // tm,),
        in_specs=[row(d), row(d), row(p.shape[1]), _const_spec(g_ple.shape),
                  _const_spec(w_gate.shape), _const_spec(w_proj.shape)],
        out_specs=row(d),
        out_shape=jax.ShapeDtypeStruct((n, d), F32),
        compiler_params=_params("parallel"),
        name="ple",
    )(h, f, p, g_ple, w_gate, w_proj)


def _final_norm(h, g, tm):
    n, d = h.shape
    row = pl.BlockSpec((tm, d), lambda i: (i, 0))
    return pl.pallas_call(
        _final_norm_kernel,
        grid=(n // tm,),
        in_specs=[row, _const_spec(g.shape)],
        out_specs=row,
        out_shape=jax.ShapeDtypeStruct((n, d), F32),
        compiler_params=_params("parallel"),
        name="final_norm",
    )(h, g)


def _rope_block(w):
    half = QK_ROPE // 2
    zero = jnp.zeros(w.shape[:-1] + (LANES // 2 - half,), w.dtype)
    return jnp.concatenate([w[..., :half], zero, w[..., half:], zero], axis=-1)


def _layout_w_in(w_in):
    off_kr = Q_LORA + KV_LORA
    off_u = off_kr + QK_ROPE
    return jnp.concatenate([w_in[:, :off_kr], w_in[:, off_u:], _rope_block(w_in[:, off_kr:off_u])],
                           axis=-1).astype(BF16)


def _layout_w_uq(w_uq):
    w = w_uq.reshape(Q_LORA, MLA_HEADS, QK_NOPE + QK_ROPE)
    w = jnp.concatenate([w[..., :QK_NOPE], _rope_block(w[..., QK_NOPE:])], axis=-1)
    return w.reshape(Q_LORA, MLA_HEADS * HEAD_PAD).astype(BF16)


def _layout_w_ukv(w_ukv):
    w = w_ukv.reshape(KV_LORA, MLA_HEADS, QK_NOPE + V_HEAD)
    k = w[..., :QK_NOPE].reshape(KV_LORA, MLA_HEADS * QK_NOPE)
    v = w[..., QK_NOPE:].reshape(KV_LORA, MLA_HEADS * V_HEAD)
    return jnp.concatenate([k, v], axis=-1).astype(BF16)


def kernel(x, p, positions, g_mix, w_in, g_cq, w_uq, g_ckv, w_ukv, g_ln_sgu, b_ln_sgu, w_sgu, b_sgu, g_out_attn, g_out_sgu, w_out, g_ffn, w_peer_q, peer_k1, peer_k2, peer_u, peer_v, g_ple, w_ple_gate, w_ple_proj, g_final):
    batch, seq, d = x.shape
    n = batch * seq
    depth = w_in.shape[0]
    pos = positions.astype(F32).reshape(n, 1)
    half = QK_ROPE // 2
    freqs = ROPE_THETA ** (-jnp.arange(half, dtype=F32) / half)
    freq = _rope_block(jnp.concatenate([freqs, freqs]))[None, :]
    sign = jnp.concatenate([-jnp.ones((LANES // 2,), F32), jnp.ones((LANES // 2,), F32)])[None, :]
    vec = lambda g: g.reshape(1, -1)

    h = x.reshape(n, d)
    for i in range(depth):
        q, k, v, u_sgu, v_sgu = _in_proj(
            h, pos, vec(g_mix[i]), _layout_w_in(w_in[i]), vec(g_cq[i]), _layout_w_uq(w_uq[i]),
            vec(g_ckv[i]), _layout_w_ukv(w_ukv[i]), freq, sign, tm=256)
        o_attn = _attention(q, k, v, batch, seq, tq=512, tk=256)
        o_sgu = _sgu(u_sgu, v_sgu, w_sgu[i], jnp.transpose(b_sgu[i]), g_ln_sgu[i], b_ln_sgu[i])
        h, a, s1, s2 = _out_proj(
            h, o_attn, o_sgu, vec(g_out_attn[i]), vec(g_out_sgu[i]), w_out[i].astype(BF16),
            vec(g_ffn[i]), w_peer_q[i].astype(BF16), peer_k1[i].astype(BF16),
            peer_k2[i].astype(BF16), tm=256)
        tau, c, e2 = _peer_topk(s1, s2, tt=256)
        f = _peer(a, peer_u[i].astype(BF16), peer_v[i].astype(BF16), s1, c, s2, e2, tau,
                  tm=512, te=1024)
        h = _ple(h, f, p[i].reshape(n, -1), vec(g_ple[i]), w_ple_gate[i].astype(BF16),
                 w_ple_proj[i].astype(BF16), tm=256)
    return _final_norm(h, vec(g_final), tm=512).reshape(batch, seq, d)
```
